```python
import jax, jax.numpy as jnp
from jax import lax
import numpy as np

D_MODEL = 2048
BATCH = 4
SEQ = 4096
DEPTH = 2

N_MIXERS = 2
N_MLA_LAYERS = (DEPTH + 1) // 2
N_CONV_LAYERS = DEPTH // 2
MLA_HEADS = 16
QK_NOPE_DIM = 128
QK_ROPE_DIM = 64
V_HEAD_DIM = 128
Q_LORA_RANK = 512
KV_LORA_RANK = 512
MLA_LATENT_DIM = Q_LORA_RANK + KV_LORA_RANK + QK_ROPE_DIM
ROPE_THETA = 10000.0
CONV_WIDTH = 3
D_FF = 4 * D_MODEL
Q_BLOCK = 128
NORM_EPS = 1e-6
N_MOD = 6

kernel_name = "hybrid_mla_shortconv_adaln_sandwich"


def rmsnorm(x, g):
    x32 = x.astype(jnp.float32)
    y = x32 * lax.rsqrt(jnp.mean(x32 * x32, axis=-1, keepdims=True) + NORM_EPS)
    return (y * g.astype(jnp.float32)).astype(x.dtype)


def rope_cos_sin(positions):
    inv_freq = ROPE_THETA ** (-jnp.arange(0, QK_ROPE_DIM, 2, dtype=jnp.float32) / QK_ROPE_DIM)
    ang = positions.astype(jnp.float32)[..., None] * inv_freq
    return jnp.cos(ang), jnp.sin(ang)


def apply_rope(t, cos, sin):
    t32 = t.astype(jnp.float32)
    half = QK_ROPE_DIM // 2
    t1, t2 = t32[..., :half], t32[..., half:]
    out = jnp.concatenate([t1 * cos - t2 * sin, t2 * cos + t1 * sin], axis=-1)
    return out.astype(t.dtype)


def mla_mixer(h, positions, w_in, g_q, g_kv, w_uq, w_ukv, w_o):
    B, S, _ = h.shape
    lat = h @ w_in
    c_q = rmsnorm(lat[..., :Q_LORA_RANK], g_q)
    c_kv = rmsnorm(lat[..., Q_LORA_RANK:Q_LORA_RANK + KV_LORA_RANK], g_kv)
    k_rope = lat[..., Q_LORA_RANK + KV_LORA_RANK:]
    cos, sin = rope_cos_sin(positions)
    k_rope = apply_rope(k_rope, cos, sin)
    q = jnp.einsum('bsr,rhd->bshd', c_q, w_uq)
    q_nope = q[..., :QK_NOPE_DIM]
    q_rope = apply_rope(q[..., QK_NOPE_DIM:], cos[:, :, None, :], sin[:, :, None, :])
    kv = jnp.einsum('bsr,rhd->bshd', c_kv, w_ukv)
    k_nope, v = kv[..., :QK_NOPE_DIM], kv[..., QK_NOPE_DIM:]

    n_blk = S // Q_BLOCK
    scale = (QK_NOPE_DIM + QK_ROPE_DIM) ** -0.5
    qn_blocks = q_nope.reshape(B, n_blk, Q_BLOCK, MLA_HEADS, QK_NOPE_DIM).transpose(1, 0, 2, 3, 4)
    qr_blocks = q_rope.reshape(B, n_blk, Q_BLOCK, MLA_HEADS, QK_ROPE_DIM).transpose(1, 0, 2, 3, 4)
    starts = jnp.arange(n_blk, dtype=jnp.int32) * Q_BLOCK
    k_idx = jnp.arange(S, dtype=jnp.int32)

    def attend(args):
        qn, qr, start = args
        s = (jnp.einsum('bqhd,bkhd->bhqk', qn, k_nope, preferred_element_type=jnp.float32)
             + jnp.einsum('bqhd,bkd->bhqk', qr, k_rope, preferred_element_type=jnp.float32)) * scale
        q_idx = start + jnp.arange(Q_BLOCK, dtype=jnp.int32)
        causal = k_idx[None, :] <= q_idx[:, None]
        s = jnp.where(causal, s, jnp.finfo(jnp.float32).min)
        p = jax.nn.softmax(s, axis=-1)
        return jnp.einsum('bhqk,bkhd->bqhd', p.astype(v.dtype), v)

    o = lax.map(attend, (qn_blocks, qr_blocks, starts))
    o = o.transpose(1, 0, 2, 3, 4).reshape(B, S, MLA_HEADS * V_HEAD_DIM)
    return o @ w_o


def short_conv_mixer(h, w_in, conv_w, w_out):
    proj = h @ w_in
    b_gate = proj[..., :D_MODEL]
    c_gate = proj[..., D_MODEL:2 * D_MODEL]
    u = proj[..., 2 * D_MODEL:]
    z = c_gate * u
    z = lax.conv_general_dilated(z, conv_w[:, None, :].astype(z.dtype), window_strides=(1,),
                                 padding=[(CONV_WIDTH - 1, 0)],
                                 dimension_numbers=('NWC', 'WIO', 'NWC'),
                                 feature_group_count=D_MODEL)
    return (b_gate * z) @ w_out


def sq_relu_mlp(h, w_up, w_down):
    a = jax.nn.relu(h @ w_up)
    return (a * a) @ w_down


def setup_inputs(seed: int = 0) -> dict:
    key = jax.random.key(seed)
    ks = jax.random.split(key, 20)
    f32 = jnp.float32

    def nrm(k, shape, fan_in, mult=1.0):
        return jax.random.normal(k, shape, f32) * (mult * fan_in ** -0.5)

    x = jax.random.normal(ks[0], (BATCH, SEQ, D_MODEL), f32)
    c = jax.random.normal(ks[1], (BATCH, D_MODEL), f32)
    offsets = jax.random.randint(ks[2], (BATCH, 1), 0, 1024, dtype=jnp.int32)
    positions = jnp.arange(SEQ, dtype=jnp.int32)[None, :] + offsets
    w_mod = nrm(ks[3], (DEPTH, D_MODEL, N_MOD * D_MODEL), D_MODEL, 0.5)
    b_mod = 0.02 * jax.random.normal(ks[4], (DEPTH, N_MOD * D_MODEL), f32)
    norm_g = 1.0 + 0.05 * jax.random.normal(ks[5], (DEPTH, 4, D_MODEL), f32)
    mla_w_in = nrm(ks[6], (N_MLA_LAYERS, D_MODEL, MLA_LATENT_DIM), D_MODEL)
    mla_g_q = 1.0 + 0.05 * jax.random.normal(ks[7], (N_MLA_LAYERS, Q_LORA_RANK), f32)
    mla_g_kv = 1.0 + 0.05 * jax.random.normal(ks[8], (N_MLA_LAYERS, KV_LORA_RANK), f32)
    mla_w_uq = nrm(ks[9], (N_MLA_LAYERS, Q_LORA_RANK, MLA_HEADS, QK_NOPE_DIM + QK_ROPE_DIM), Q_LORA_RANK)
    mla_w_ukv = nrm(ks[10], (N_MLA_LAYERS, KV_LORA_RANK, MLA_HEADS, QK_NOPE_DIM + V_HEAD_DIM), KV_LORA_RANK)
    mla_w_o = nrm(ks[11], (N_MLA_LAYERS, MLA_HEADS * V_HEAD_DIM, D_MODEL), MLA_HEADS * V_HEAD_DIM)
    conv_w_in = nrm(ks[12], (N_CONV_LAYERS, D_MODEL, 3 * D_MODEL), D_MODEL)
    conv_w = nrm(ks[13], (N_CONV_LAYERS, CONV_WIDTH, D_MODEL), CONV_WIDTH)
    conv_w_out = nrm(ks[14], (N_CONV_LAYERS, D_MODEL, D_MODEL), D_MODEL)
    mlp_w_up = nrm(ks[15], (DEPTH, D_MODEL, D_FF), D_MODEL)
    mlp_w_down = nrm(ks[16], (DEPTH, D_FF, D_MODEL), D_FF)
    return {"x": x, "c": c, "positions": positions, "w_mod": w_mod, "b_mod": b_mod,
            "norm_g": norm_g, "mla_w_in": mla_w_in, "mla_g_q": mla_g_q, "mla_g_kv": mla_g_kv,
            "mla_w_uq": mla_w_uq, "mla_w_ukv": mla_w_ukv, "mla_w_o": mla_w_o,
            "conv_w_in": conv_w_in, "conv_w": conv_w, "conv_w_out": conv_w_out,
            "mlp_w_up": mlp_w_up, "mlp_w_down": mlp_w_down}


def reference(x, c, positions, w_mod, b_mod, norm_g, mla_w_in, mla_g_q, mla_g_kv,
              mla_w_uq, mla_w_ukv, mla_w_o, conv_w_in, conv_w, conv_w_out,
              mlp_w_up, mlp_w_down):
    cond = jax.nn.silu(c)
    for i in range(DEPTH):
        mod = (cond @ w_mod[i] + b_mod[i])[:, None, :]
        sh1, sc1, g1, sh2, sc2, g2 = jnp.split(mod, N_MOD, axis=-1)
        h = rmsnorm(x, norm_g[i, 0]) * (1.0 + sc1) + sh1
        j = i // N_MIXERS
        if i % N_MIXERS == 0:
            y = mla_mixer(h, positions, mla_w_in[j], mla_g_q[j], mla_g_kv[j],
                          mla_w_uq[j], mla_w_ukv[j], mla_w_o[j])
        else:
            y = short_conv_mixer(h, conv_w_in[j], conv_w[j], conv_w_out[j])
        x = x + g1 * rmsnorm(y, norm_g[i, 1])
        h = rmsnorm(x, norm_g[i, 2]) * (1.0 + sc2) + sh2
        y = sq_relu_mlp(h, mlp_w_up[i], mlp_w_down[i])
        x = x + g2 * rmsnorm(y, norm_g[i, 3])
    return x
```

```python
import functools
import math

import jax
import jax.numpy as jnp
from jax import lax
from jax.experimental import pallas as pl
from jax.experimental.pallas import tpu as pltpu

F32 = jnp.float32
BF16 = jnp.bfloat16

N_MIXERS = 2
N_MOD = 6
NORM_EPS = 1e-6
ROPE_THETA = 10000.0
CONV_WIDTH = 3
LANES = 128
SUBLANES = 8
MASK_VALUE = -1e30

TM_PROJ = 512
TM_OUT = 512
TM_MLP = 512
TF_MLP = 512
TM_CONV = 512
TC_CONV = 512
TQ_ATTN = 512
TN_MOD = 1024
TM_ROPE = 2048
VMEM_LIMIT = 56 * 1024 * 1024


def _dot(a, b):
    return jnp.dot(a, b, preferred_element_type=F32)


def _rms(x, g):
    return x * lax.rsqrt(jnp.mean(x * x, axis=-1, keepdims=True) + NORM_EPS) * g


def _params(*sem):
    return pltpu.CompilerParams(dimension_semantics=sem, vmem_limit_bytes=VMEM_LIMIT)


def _mod_kernel(c_ref, w_ref, b_ref, o_ref):
    c = c_ref[...]
    cond = c / (1.0 + jnp.exp(-c))
    o_ref[...] = _dot(cond.astype(BF16), w_ref[...].astype(BF16)) + b_ref[...]


def _modulation(c, w_mod, b_mod):
    depth, d, n = w_mod.shape
    b = c.shape[0]
    return pl.pallas_call(
        _mod_kernel,
        grid=(depth, n // TN_MOD),
        in_specs=[
            pl.BlockSpec((b, d), lambda l, j: (0, 0)),
            pl.BlockSpec((None, d, TN_MOD), lambda l, j: (l, 0, j)),
            pl.BlockSpec((None, 1, TN_MOD), lambda l, j: (l, 0, j)),
        ],
        out_specs=pl.BlockSpec((None, b, TN_MOD), lambda l, j: (l, 0, j)),
        out_shape=jax.ShapeDtypeStruct((depth, b, n), F32),
        compiler_params=_params("arbitrary", "arbitrary"),
        name="adaln_mod",
    )(c, w_mod, b_mod.reshape(depth, 1, n))


def _rope_kernel(pos_ref, invf_ref, cos_ref, sin_ref):
    ang = pos_ref[...].astype(F32) * invf_ref[...]
    cos_ref[...] = jnp.cos(ang)
    sin_ref[...] = jnp.sin(ang)


def _rope_tables(positions, rope_dim):
    t = positions.size
    half = rope_dim // 2
    inv_freq = ROPE_THETA ** (-jnp.arange(0, rope_dim, 2, dtype=F32) / rope_dim)
    inv_freq = jnp.tile(inv_freq, LANES // half).reshape(1, LANES)
    out = jax.ShapeDtypeStruct((t, LANES), F32)
    return pl.pallas_call(
        _rope_kernel,
        grid=(t // TM_ROPE,),
        in_specs=[
            pl.BlockSpec((TM_ROPE, 1), lambda i: (i, 0)),
            pl.BlockSpec((1, LANES), lambda i: (0, 0)),
        ],
        out_specs=[pl.BlockSpec((TM_ROPE, LANES), lambda i: (i, 0))] * 2,
        out_shape=[out, out],
        compiler_params=_params("arbitrary"),
        name="rope_tables",
    )(positions.reshape(t, 1), inv_freq)


def _mla_proj_kernel(x_ref, mod_ref, ng_ref, win_ref, gq_ref, gkv_ref, wq_ref, wkv_ref,
                     cos_ref, sin_ref, qn_ref, qr_ref, kn_ref, v_ref, kr_ref,
                     *, q_rank, kv_rank, nope_w, rope_w, q_scale, chunk):
    sh, sc = mod_ref[0:1, :], mod_ref[1:2, :]
    h = (_rms(x_ref[...], ng_ref[0:1, :]) * (1.0 + sc) + sh).astype(BF16)
    lat = _dot(h, win_ref[...])
    cq = _rms(lat[:, :q_rank], gq_ref[...]).astype(BF16)
    ckv = _rms(lat[:, q_rank:q_rank + kv_rank], gkv_ref[...]).astype(BF16)
    cos, sin = cos_ref[...], sin_ref[...]
    o = q_rank + kv_rank
    krr = lat[:, o:o + LANES] * cos + lat[:, o + LANES:o + 2 * LANES] * sin
    lane = lax.broadcasted_iota(jnp.int32, krr.shape, 1)
    zero = jnp.zeros_like(krr)
    kr_ref[:, :LANES] = jnp.where(lane < LANES // 2, krr, zero).astype(BF16)
    kr_ref[:, LANES:] = jnp.where(lane >= LANES // 2, krr, zero).astype(BF16)
    for j in range(nope_w // chunk):
        cs = slice(j * chunk, (j + 1) * chunk)
        qn_ref[:, cs] = (_dot(cq, wq_ref[:, cs]) * q_scale).astype(BF16)
    for j in range(rope_w // LANES):
        a = _dot(cq, wq_ref[:, nope_w + j * LANES:nope_w + (j + 1) * LANES])
        r = _dot(cq, wq_ref[:, nope_w + rope_w + j * LANES:nope_w + rope_w + (j + 1) * LANES])
        qr_ref[:, j * LANES:(j + 1) * LANES] = ((a * cos + r * sin) * q_scale).astype(BF16)
    for j in range(nope_w // chunk):
        cs = slice(j * chunk, (j + 1) * chunk)
        kn_ref[:, cs] = _dot(ckv, wkv_ref[:, cs]).astype(BF16)
        v_ref[:, cs] = _dot(ckv, wkv_ref[:, nope_w + j * chunk:nope_w + (j + 1) * chunk]).astype(BF16)


def _rot_half_weight(w):
    half = w.shape[-1] // 2
    return jnp.concatenate([-w[..., half:], w[..., :half]], axis=-1)


def _mla_proj(x2, mod, ng, w_in, g_q, g_kv, w_uq, w_ukv, cos, sin, layer, batch):
    t, d = x2.shape
    q_rank, heads, qk_dim = w_uq.shape
    kv_rank = w_ukv.shape[0]
    rope = w_in.shape[1] - q_rank - kv_rank
    nope = qk_dim - rope
    vdim = w_ukv.shape[2] - nope
    assert nope == LANES and vdim == LANES and 2 * rope == LANES
    nope_w, rope_w = heads * nope, heads * rope
    w_kr = w_in[:, q_rank + kv_rank:]
    w_kr_r = _rot_half_weight(w_kr)
    win = jnp.concatenate([w_in[:, :q_rank + kv_rank], w_kr, w_kr, w_kr_r, w_kr_r], axis=1).astype(BF16)
    wq_r = w_uq[:, :, nope:]
    wq = jnp.concatenate([w_uq[:, :, :nope].reshape(q_rank, nope_w),
                          wq_r.reshape(q_rank, rope_w),
                          _rot_half_weight(wq_r).reshape(q_rank, rope_w)], axis=1).astype(BF16)
    wkv = jnp.concatenate([w_ukv[:, :, :nope].reshape(kv_rank, nope_w),
                           w_ukv[:, :, nope:].reshape(kv_rank, nope_w)], axis=1).astype(BF16)
    tm = TM_PROJ
    per_seq = (t // batch) // tm
    q_scale = float(qk_dim) ** -0.5 * math.log2(math.e)
    full = lambda a: pl.BlockSpec(a.shape, lambda i: (0,) * a.ndim)
    row = lambda w: pl.BlockSpec((tm, w), lambda i: (i, 0))
    kern = functools.partial(_mla_proj_kernel, q_rank=q_rank, kv_rank=kv_rank, nope_w=nope_w,
                             rope_w=rope_w, q_scale=q_scale, chunk=512)
    g_q2, g_kv2 = g_q.reshape(1, -1), g_kv.reshape(1, -1)
    return pl.pallas_call(
        kern,
        grid=(t // tm,),
        in_specs=[
            row(d),
            pl.BlockSpec((None, N_MOD, d), lambda i: (layer * batch + i // per_seq, 0, 0)),
            pl.BlockSpec((None, 4, d), lambda i: (layer, 0, 0)),
            full(win), full(g_q2), full(g_kv2), full(wq), full(wkv),
            row(LANES), row(LANES),
        ],
        out_specs=[row(nope_w), row(rope_w), row(nope_w), row(nope_w), row(2 * LANES)],
        out_shape=[jax.ShapeDtypeStruct((t, nope_w), BF16), jax.ShapeDtypeStruct((t, rope_w), BF16),
                   jax.ShapeDtypeStruct((t, nope_w), BF16), jax.ShapeDtypeStruct((t, nope_w), BF16),
                   jax.ShapeDtypeStruct((t, 2 * LANES), BF16)],
        compiler_params=_params("arbitrary"),
        name="mla_proj",
    )(x2, mod, ng, win, g_q2, g_kv2, wq, wkv, cos, sin)


def _attn_kernel(qn_ref, qr_ref, kn_ref, kr_ref, v_ref, o_ref, kcat_ref, m_ref, l_ref, acc_ref, *, tq):
    qi = pl.program_id(2)

    @pl.when(qi == 0)
    def _():
        for e in range(2):
            kcat_ref[e, :, :LANES] = kn_ref[:, e * LANES:(e + 1) * LANES]
            kcat_ref[e, :, LANES:] = kr_ref[:, e * LANES:(e + 1) * LANES]

    qr = qr_ref[...]
    qs = [jnp.concatenate([qn_ref[:, e * LANES:(e + 1) * LANES], qr], axis=1) for e in range(2)]
    m_ref[...] = jnp.full(m_ref.shape, MASK_VALUE, F32)
    l_ref[...] = jnp.zeros(l_ref.shape, F32)
    acc_ref[...] = jnp.zeros(acc_ref.shape, F32)

    def step(j, masked):
        off = pl.multiple_of(j * tq, tq)
        for e in range(2):
            k = kcat_ref[e, pl.ds(off, tq), :]
            s = lax.dot_general(qs[e], k, (((1,), (1,)), ((), ())), preferred_element_type=F32)
            if masked:
                row = lax.broadcasted_iota(jnp.int32, s.shape, 0)
                col = lax.broadcasted_iota(jnp.int32, s.shape, 1)
                s = jnp.where(col <= row, s, MASK_VALUE)
            m_prev = m_ref[e]
            m_new = jnp.maximum(m_prev, jnp.max(s, axis=1, keepdims=True))
            alpha = jnp.exp2(m_prev - m_new)
            p = jnp.exp2(s - pltpu.repeat(m_new, tq // LANES, axis=1))
            l_ref[e] = alpha * l_ref[e] + jnp.sum(p, axis=1, keepdims=True)
            pv = _dot(p.astype(BF16), v_ref[pl.ds(off, tq), e * LANES:(e + 1) * LANES])
            acc_ref[e] = acc_ref[e] * alpha + pv
            m_ref[e] = m_new

    def body(j, carry):
        step(j, False)
        return carry

    lax.fori_loop(0, qi, body, 0)
    step(qi, True)
    for e in range(2):
        o_ref[:, e * LANES:(e + 1) * LANES] = (acc_ref[e] / l_ref[e]).astype(BF16)


def _attention(qn, qr, kn, kr, v, batch):
    t, hw = qn.shape
    s = t // batch
    tq = TQ_ATTN
    nq = s // tq
    pair = 2 * LANES
    return pl.pallas_call(
        functools.partial(_attn_kernel, tq=tq),
        grid=(batch, hw // pair, nq),
        in_specs=[
            pl.BlockSpec((tq, pair), lambda b, p, i: (b * nq + i, p)),
            pl.BlockSpec((tq, LANES), lambda b, p, i: (b * nq + i, p)),
            pl.BlockSpec((s, pair), lambda b, p, i: (b, p)),
            pl.BlockSpec((s, pair), lambda b, p, i: (b, 0)),
            pl.BlockSpec((s, pair), lambda b, p, i: (b, p)),
        ],
        out_specs=pl.BlockSpec((tq, pair), lambda b, p, i: (b * nq + i, p)),
        out_shape=jax.ShapeDtypeStruct((t, hw), BF16),
        scratch_shapes=[
            pltpu.VMEM((2, s, pair), BF16),
            pltpu.VMEM((2, tq, LANES), F32),
            pltpu.VMEM((2, tq, LANES), F32),
            pltpu.VMEM((2, tq, LANES), F32),
        ],
        compiler_params=_params("arbitrary", "arbitrary", "arbitrary"),
        name="mla_attention",
    )(qn, qr, kn, kr, v)


def _residual(y, x, gate, g_post):
    return x + gate * _rms(y, g_post)


def _prenorm(x, g_pre, sc, sh):
    return (_rms(x, g_pre) * (1.0 + sc) + sh).astype(BF16)


def _out_proj_kernel(a_ref, w_ref, x_ref, mod_ref, ng_ref, xo_ref, ho_ref):
    y = _dot(a_ref[...], w_ref[...])
    x1 = _residual(y, x_ref[...], mod_ref[2:3, :], ng_ref[1:2, :])
    xo_ref[...] = x1
    ho_ref[...] = _prenorm(x1, ng_ref[2:3, :], mod_ref[4:5, :], mod_ref[3:4, :])


def _out_proj(a, w, x2, mod, ng, layer, batch):
    t, d = x2.shape
    tm = TM_OUT
    per_seq = (t // batch) // tm
    wb = w.astype(BF16)
    row = lambda width: pl.BlockSpec((tm, width), lambda i: (i, 0))
    return pl.pallas_call(
        _out_proj_kernel,
        grid=(t // tm,),
        in_specs=[
            row(a.shape[1]),
            pl.BlockSpec(wb.shape, lambda i: (0, 0)),
            row(d),
            pl.BlockSpec((None, N_MOD, d), lambda i: (layer * batch + i // per_seq, 0, 0)),
            pl.BlockSpec((None, 4, d), lambda i: (layer, 0, 0)),
        ],
        out_specs=[row(d), row(d)],
        out_shape=[jax.ShapeDtypeStruct((t, d), F32), jax.ShapeDtypeStruct((t, d), BF16)],
        compiler_params=_params("arbitrary"),
        name="mixer_out_proj",
    )(a, wb, x2, mod, ng)


def _mlp_kernel(*refs, has_next):
    if has_next:
        h_ref, wu_ref, wd_ref, x_ref, mod_ref, ng_ref, modn_ref, ngn_ref, xo_ref, ho_ref, acc_ref = refs
    else:
        h_ref, wu_ref, wd_ref, x_ref, mod_ref, ng_ref, xo_ref, acc_ref = refs
    k = pl.program_id(1)
    a = jnp.maximum(_dot(h_ref[...], wu_ref[...]), 0.0)
    part = _dot((a * a).astype(BF16), wd_ref[...])

    @pl.when(k == 0)
    def _():
        acc_ref[...] = part

    @pl.when(k > 0)
    def _():
        acc_ref[...] += part

    @pl.when(k == pl.num_programs(1) - 1)
    def _():
        x2 = _residual(acc_ref[...], x_ref[...], mod_ref[5:6, :], ng_ref[3:4, :])
        xo_ref[...] = x2
        if has_next:
            ho_ref[...] = _prenorm(x2, ngn_ref[0:1, :], modn_ref[1:2, :], modn_ref[0:1, :])


def _mlp(h, w_up, w_down, x2, mod, ng, layer, batch, has_next):
    t, d = x2.shape
    d_ff = w_up.shape[1]
    tm, tf = TM_MLP, TF_MLP
    per_seq = (t // batch) // tm
    wu, wd = w_up.astype(BF16), w_down.astype(BF16)
    row = lambda: pl.BlockSpec((tm, d), lambda i, k: (i, 0))
    mod_spec = lambda l: pl.BlockSpec((None, N_MOD, d), lambda i, k: (l * batch + i // per_seq, 0, 0))
    ng_spec = lambda l: pl.BlockSpec((None, 4, d), lambda i, k: (l, 0, 0))
    in_specs = [
        row(),
        pl.BlockSpec((d, tf), lambda i, k: (0, k)),
        pl.BlockSpec((tf, d), lambda i, k: (k, 0)),
        row(), mod_spec(layer), ng_spec(layer),
    ]
    args = [h, wu, wd, x2, mod, ng]
    out_specs = [row()]
    out_shape = [jax.ShapeDtypeStruct((t, d), F32)]
    if has_next:
        in_specs += [mod_spec(layer + 1), ng_spec(layer + 1)]
        args += [mod, ng]
        out_specs.append(row())
        out_shape.append(jax.ShapeDtypeStruct((t, d), BF16))
    return pl.pallas_call(
        functools.partial(_mlp_kernel, has_next=has_next),
        grid=(t // tm, d_ff // tf),
        in_specs=in_specs,
        out_specs=out_specs,
        out_shape=out_shape,
        scratch_shapes=[pltpu.VMEM((tm, d), F32)],
        compiler_params=_params("arbitrary", "arbitrary"),
        name="sq_relu_mlp",
    )(*args)


def _conv_kernel(h_ref, w_ref, cw_ref, g_ref, carry_ref, *, d, per_seq, chunk):
    i = pl.program_id(0)
    h = h_ref[...]
    tm = h.shape[0]
    @pl.when(i % per_seq == 0)
    def _():
        carry_ref[...] = jnp.zeros(carry_ref.shape, F32)

    row = lax.broadcasted_iota(jnp.int32, (tm, chunk), 0)
    for j in range(d // chunk):
        cs = slice(j * chunk, (j + 1) * chunk)
        b_gate = _dot(h, w_ref[:, cs])
        c_gate = _dot(h, w_ref[:, d + j * chunk:d + (j + 1) * chunk])
        u = _dot(h, w_ref[:, 2 * d + j * chunk:2 * d + (j + 1) * chunk])
        z = c_gate * u
        prev = carry_ref[:, cs]
        carry_ref[:, cs] = z[tm - SUBLANES:, :]
        p1 = prev[SUBLANES - 1:SUBLANES, :]
        p2 = prev[SUBLANES - 2:SUBLANES - 1, :]
        z1 = jnp.where(row == 0, p1, pltpu.roll(z, 1, 0))
        z2 = jnp.where(row == 0, p2, jnp.where(row == 1, p1, pltpu.roll(z, 2, 0)))
        conv = cw_ref[0:1, cs] * z2 + cw_ref[1:2, cs] * z1 + cw_ref[2:3, cs] * z
        g_ref[:, cs] = (b_gate * conv).astype(BF16)


def _conv_front(h, w_in, conv_w, batch):
    t, d = h.shape
    tm = TM_CONV
    per_seq = (t // batch) // tm
    wb = w_in.astype(BF16)
    return pl.pallas_call(
        functools.partial(_conv_kernel, d=d, per_seq=per_seq, chunk=TC_CONV),
        grid=(t // tm,),
        in_specs=[
            pl.BlockSpec((tm, d), lambda i: (i, 0)),
            pl.BlockSpec(wb.shape, lambda i: (0, 0)),
            pl.BlockSpec(conv_w.shape, lambda i: (0, 0)),
        ],
        out_specs=pl.BlockSpec((tm, d), lambda i: (i, 0)),
        out_shape=jax.ShapeDtypeStruct((t, d), BF16),
        scratch_shapes=[pltpu.VMEM((SUBLANES, d), F32)],
        compiler_params=_params("arbitrary"),
        name="short_conv_front",
    )(h, wb, conv_w)


def _prenorm_kernel(x_ref, mod_ref, ng_ref, h_ref):
    h_ref[...] = _prenorm(x_ref[...], ng_ref[0:1, :], mod_ref[1:2, :], mod_ref[0:1, :])


def _first_prenorm(x2, mod, ng, layer, batch):
    t, d = x2.shape
    tm = TM_OUT
    per_seq = (t // batch) // tm
    return pl.pallas_call(
        _prenorm_kernel,
        grid=(t // tm,),
        in_specs=[
            pl.BlockSpec((tm, d), lambda i: (i, 0)),
            pl.BlockSpec((None, N_MOD, d), lambda i: (layer * batch + i // per_seq, 0, 0)),
            pl.BlockSpec((None, 4, d), lambda i: (layer, 0, 0)),
        ],
        out_specs=pl.BlockSpec((tm, d), lambda i: (i, 0)),
        out_shape=jax.ShapeDtypeStruct((t, d), BF16),
        compiler_params=_params("arbitrary"),
        name="first_prenorm",
    )(x2, mod, ng)


def kernel(x, c, positions, w_mod, b_mod, norm_g, mla_w_in, mla_g_q, mla_g_kv, mla_w_uq, mla_w_ukv,
           mla_w_o, conv_w_in, conv_w, conv_w_out, mlp_w_up, mlp_w_down):
    batch, seq, d = x.shape
    depth = w_mod.shape[0]
    t = batch * seq
    mod = _modulation(c, w_mod, b_mod).reshape(depth * batch, N_MOD, d)
    rope_dim = mla_w_in.shape[2] - mla_w_uq.shape[1] - mla_w_ukv.shape[1]
    cos, sin = _rope_tables(positions, rope_dim)
    x2 = x.reshape(t, d)
    h = None
    for i in range(depth):
        j = i // N_MIXERS
        if i % N_MIXERS == 0:
            qn, qr, kn, v, kr = _mla_proj(x2, mod, norm_g, mla_w_in[j], mla_g_q[j], mla_g_kv[j],
                                          mla_w_uq[j], mla_w_ukv[j], cos, sin, i, batch)
            a = _attention(qn, qr, kn, kr, v, batch)
            w_out = mla_w_o[j]
        else:
            if h is None:
                h = _first_prenorm(x2, mod, norm_g, i, batch)
            a = _conv_front(h, conv_w_in[j], conv_w[j], batch)
            w_out = conv_w_out[j]
        x2, h = _out_proj(a, w_out, x2, mod, norm_g, i, batch)
        has_next = i + 1 < depth and (i + 1) % N_MIXERS == 1
        outs = _mlp(h, mlp_w_up[i], mlp_w_down[i], x2, mod, norm_g, i, batch, has_next)
        x2 = outs[0]
        h = outs[1] if has_next else None
    return x2.reshape(batch, seq, d)
```

```python
import functools
import math

import jax
import jax.numpy as jnp
from jax import lax
from jax.experimental import pallas as pl
from jax.experimental.pallas import tpu as pltpu

F32 = jnp.float32
BF16 = jnp.bfloat16

N_MIXERS = 2
N_MOD = 6
NORM_EPS = 1e-6
ROPE_THETA = 10000.0
CONV_WIDTH = 3
LANES = 128
SUBLANES = 8
BF16_ROWS = 16
MASK_VALUE = -1e30

TM_PROJ = 512
TM_OUT = 512
SUB_OUT = 256
TM_MLP = 512
TF_MLP = 1024
TM_CONV = 512
TC_CONV = 512
NSUB_ATTN = 2
TN_MOD = 1024
TM_ROPE = 2048
VMEM_LIMIT = 56 * 1024 * 1024


def _dot(a, b):
    return jnp.dot(a, b, preferred_element_type=F32)


def _dot_nt(a, b):
    return lax.dot_general(a, b, (((1,), (1,)), ((), ())), preferred_element_type=F32)


def _rms(x, g):
    return x * lax.rsqrt(jnp.mean(x * x, axis=-1, keepdims=True) + NORM_EPS) * g


def _params(*sem):
    return pltpu.CompilerParams(dimension_semantics=sem, vmem_limit_bytes=VMEM_LIMIT)


def _mod_kernel(c_ref, w_ref, b_ref, o_ref):
    c = c_ref[...]
    cond = c / (1.0 + jnp.exp(-c))
    o_ref[...] = _dot(cond.astype(BF16), w_ref[...].astype(BF16)) + b_ref[...]


def _modulation(c, w_mod, b_mod):
    depth, d, n = w_mod.shape
    b = c.shape[0]
    return pl.pallas_call(
        _mod_kernel,
        grid=(depth, n // TN_MOD),
        in_specs=[
            pl.BlockSpec((b, d), lambda l, j: (0, 0)),
            pl.BlockSpec((None, d, TN_MOD), lambda l, j: (l, 0, j)),
            pl.BlockSpec((None, 1, TN_MOD), lambda l, j: (l, 0, j)),
        ],
        out_specs=pl.BlockSpec((None, b, TN_MOD), lambda l, j: (l, 0, j)),
        out_shape=jax.ShapeDtypeStruct((depth, b, n), F32),
        compiler_params=_params("arbitrary", "arbitrary"),
        name="adaln_mod",
    )(c, w_mod, b_mod.reshape(depth, 1, n))


def _rope_kernel(pos_ref, invf_ref, cos_ref, sin_ref):
    ang = pos_ref[...].astype(F32) * invf_ref[...]
    cos_ref[...] = jnp.cos(ang)
    sin_ref[...] = jnp.sin(ang)


def _rope_tables(positions, rope_dim):
    t = positions.size
    half = rope_dim // 2
    inv_freq = ROPE_THETA ** (-jnp.arange(0, rope_dim, 2, dtype=F32) / rope_dim)
    inv_freq = jnp.tile(inv_freq, LANES // half).reshape(1, LANES)
    out = jax.ShapeDtypeStruct((t, LANES), F32)
    return pl.pallas_call(
        _rope_kernel,
        grid=(t // TM_ROPE,),
        in_specs=[
            pl.BlockSpec((TM_ROPE, 1), lambda i: (i, 0)),
            pl.BlockSpec((1, LANES), lambda i: (0, 0)),
        ],
        out_specs=[pl.BlockSpec((TM_ROPE, LANES), lambda i: (i, 0))] * 2,
        out_shape=[out, out],
        compiler_params=_params("arbitrary"),
        name="rope_tables",
    )(positions.reshape(t, 1), inv_freq)


def _mla_proj_kernel(x_ref, mod_ref, ng_ref, win_ref, gq_ref, gkv_ref, wq_ref, wkn_ref, wvt_ref,
                     cos_ref, sin_ref, qn_ref, qr_ref, kn_ref, vt_ref, kr_ref,
                     *, q_rank, kv_rank, nope_w, rope_w, q_scale, chunk):
    sh, sc = mod_ref[0:1, :], mod_ref[1:2, :]
    h = (_rms(x_ref[...], ng_ref[0:1, :]) * (1.0 + sc) + sh).astype(BF16)
    lat = _dot(h, win_ref[...])
    cq = _rms(lat[:, :q_rank], gq_ref[...]).astype(BF16)
    ckv = _rms(lat[:, q_rank:q_rank + kv_rank], gkv_ref[...]).astype(BF16)
    cos, sin = cos_ref[...], sin_ref[...]
    o = q_rank + kv_rank
    krr = lat[:, o:o + LANES] * cos + lat[:, o + LANES:o + 2 * LANES] * sin
    lane = lax.broadcasted_iota(jnp.int32, krr.shape, 1)
    zero = jnp.zeros_like(krr)
    kr_ref[:, :LANES] = jnp.where(lane < LANES // 2, krr, zero).astype(BF16)
    kr_ref[:, LANES:] = jnp.where(lane >= LANES // 2, krr, zero).astype(BF16)
    for j in range(nope_w // chunk):
        cs = slice(j * chunk, (j + 1) * chunk)
        qn_ref[:, cs] = (_dot(cq, wq_ref[:, cs]) * q_scale).astype(BF16)
    for j in range(rope_w // LANES):
        a = _dot(cq, wq_ref[:, nope_w + j * LANES:nope_w + (j + 1) * LANES])
        r = _dot(cq, wq_ref[:, nope_w + rope_w + j * LANES:nope_w + rope_w + (j + 1) * LANES])
        qr_ref[:, j * LANES:(j + 1) * LANES] = ((a * cos + r * sin) * q_scale).astype(BF16)
    for j in range(nope_w // chunk):
        cs = slice(j * chunk, (j + 1) * chunk)
        kn_ref[:, cs] = _dot(ckv, wkn_ref[:, cs]).astype(BF16)
        vt_ref[cs, :] = _dot_nt(wvt_ref[cs, :], ckv).astype(BF16)


def _rot_half_weight(w):
    half = w.shape[-1] // 2
    return jnp.concatenate([-w[..., half:], w[..., :half]], axis=-1)


def _mla_proj(x2, mod, ng, w_in, g_q, g_kv, w_uq, w_ukv, cos, sin, layer, batch):
    t, d = x2.shape
    q_rank, heads, qk_dim = w_uq.shape
    kv_rank = w_ukv.shape[0]
    rope = w_in.shape[1] - q_rank - kv_rank
    nope = qk_dim - rope
    vdim = w_ukv.shape[2] - nope
    assert nope == LANES and vdim == LANES and 2 * rope == LANES
    nope_w, rope_w = heads * nope, heads * rope
    w_kr = w_in[:, q_rank + kv_rank:]
    w_kr_r = _rot_half_weight(w_kr)
    win = jnp.concatenate([w_in[:, :q_rank + kv_rank], w_kr, w_kr, w_kr_r, w_kr_r], axis=1).astype(BF16)
    wq_r = w_uq[:, :, nope:]
    wq = jnp.concatenate([w_uq[:, :, :nope].reshape(q_rank, nope_w),
                          wq_r.reshape(q_rank, rope_w),
                          _rot_half_weight(wq_r).reshape(q_rank, rope_w)], axis=1).astype(BF16)
    wkn = w_ukv[:, :, :nope].reshape(kv_rank, nope_w).astype(BF16)
    wvt = w_ukv[:, :, nope:].reshape(kv_rank, nope_w).T.astype(BF16)
    tm = TM_PROJ
    per_seq = (t // batch) // tm
    q_scale = float(qk_dim) ** -0.5 * math.log2(math.e)
    full = lambda a: pl.BlockSpec(a.shape, lambda i: (0,) * a.ndim)
    row = lambda w: pl.BlockSpec((tm, w), lambda i: (i, 0))
    kern = functools.partial(_mla_proj_kernel, q_rank=q_rank, kv_rank=kv_rank, nope_w=nope_w,
                             rope_w=rope_w, q_scale=q_scale, chunk=512)
    g_q2, g_kv2 = g_q.reshape(1, -1), g_kv.reshape(1, -1)
    return pl.pallas_call(
        kern,
        grid=(t // tm,),
        in_specs=[
            row(d),
            pl.BlockSpec((None, N_MOD, d), lambda i: (layer * batch + i // per_seq, 0, 0)),
            pl.BlockSpec((None, 4, d), lambda i: (layer, 0, 0)),
            full(win), full(g_q2), full(g_kv2), full(wq), full(wkn), full(wvt),
            row(LANES), row(LANES),
        ],
        out_specs=[row(nope_w), row(rope_w), row(nope_w),
                   pl.BlockSpec((None, nope_w, tm), lambda i: (i, 0, 0)), row(2 * LANES)],
        out_shape=[jax.ShapeDtypeStruct((t, nope_w), BF16), jax.ShapeDtypeStruct((t, rope_w), BF16),
                   jax.ShapeDtypeStruct((t, nope_w), BF16), jax.ShapeDtypeStruct((t // tm, nope_w, tm), BF16),
                   jax.ShapeDtypeStruct((t, 2 * LANES), BF16)],
        compiler_params=_params("arbitrary"),
        name="mla_proj",
    )(x2, mod, ng, win, g_q2, g_kv2, wq, wkn, wvt, cos, sin)


def _attn_kernel(qn_ref, qr_ref, kn_ref, kr_ref, vt_ref, o_ref, kcat_ref, vaug_ref, m_ref, acc_ref, *, tk, nsub):
    qi = pl.program_id(2)
    aug = vaug_ref.shape[2]

    @pl.when(qi == 0)
    def _():
        pad = (lax.broadcasted_iota(jnp.int32, (aug - LANES, tk), 0) == 0).astype(BF16)
        for e in range(2):
            kcat_ref[e, :, :LANES] = kn_ref[:, e * LANES:(e + 1) * LANES]
            kcat_ref[e, :, LANES:] = kr_ref[:, e * LANES:(e + 1) * LANES]
            for t in range(vt_ref.shape[0]):
                vaug_ref[e, t, :LANES, :] = vt_ref[t, e * LANES:(e + 1) * LANES, :]
                vaug_ref[e, t, LANES:, :] = pad

    qr = qr_ref[...]
    qs = [jnp.concatenate([qn_ref[:, e * LANES:(e + 1) * LANES], qr], axis=1) for e in range(2)]
    m_ref[...] = jnp.full(m_ref.shape, MASK_VALUE, F32)
    acc_ref[...] = jnp.zeros(acc_ref.shape, F32)

    def step(j, modes):
        off = pl.multiple_of(j * tk, tk)
        chains = [(e, u) for e in range(2) for u in range(nsub) if modes[u] is not None]
        scores = [_dot_nt(kcat_ref[e, pl.ds(off, tk), :], qs[e][u * tk:(u + 1) * tk, :]) for e, u in chains]
        for (e, u), s in zip(chains, scores):
            cs = slice(u * tk, (u + 1) * tk)
            if modes[u] == "diag":
                key = lax.broadcasted_iota(jnp.int32, s.shape, 0)
                qry = lax.broadcasted_iota(jnp.int32, s.shape, 1)
                s = jnp.where(key <= qry, s, MASK_VALUE)
            m_prev = m_ref[e, :, cs]
            m_new = jnp.maximum(m_prev, jnp.max(s, axis=0, keepdims=True))
            alpha = jnp.exp2(m_prev - m_new)
            p = jnp.exp2(s - m_new).astype(BF16)
            acc_ref[e, :, cs] = acc_ref[e, :, cs] * alpha + _dot(vaug_ref[e, j], p)
            m_ref[e, :, cs] = m_new

    def body(j, carry):
        step(j, ["full"] * nsub)
        return carry

    lax.fori_loop(0, qi * nsub, body, 0)
    for d in range(nsub):
        step(qi * nsub + d, [None if u < d else ("diag" if u == d else "full") for u in range(nsub)])
    for e in range(2):
        o_t = acc_ref[e, :LANES, :] / acc_ref[e, LANES:LANES + 1, :]
        o_ref[:, e * LANES:(e + 1) * LANES] = o_t.T.astype(BF16)


def _attention(qn, qr, kn, kr, vt, batch):
    t, hw = qn.shape
    s = t // batch
    tk = vt.shape[2]
    nsub = NSUB_ATTN
    tq = nsub * tk
    nq, nk = s // tq, s // tk
    pair = 2 * LANES
    aug = LANES + BF16_ROWS
    return pl.pallas_call(
        functools.partial(_attn_kernel, tk=tk, nsub=nsub),
        grid=(batch, hw // pair, nq),
        in_specs=[
            pl.BlockSpec((tq, pair), lambda b, p, i: (b * nq + i, p)),
            pl.BlockSpec((tq, LANES), lambda b, p, i: (b * nq + i, p)),
            pl.BlockSpec((s, pair), lambda b, p, i: (b, p)),
            pl.BlockSpec((s, pair), lambda b, p, i: (b, 0)),
            pl.BlockSpec((nk, pair, tk), lambda b, p, i: (b, p, 0)),
        ],
        out_specs=pl.BlockSpec((tq, pair), lambda b, p, i: (b * nq + i, p)),
        out_shape=jax.ShapeDtypeStruct((t, hw), BF16),
        scratch_shapes=[
            pltpu.VMEM((2, s, pair), BF16),
            pltpu.VMEM((2, nk, aug, tk), BF16),
            pltpu.VMEM((2, 1, tq), F32),
            pltpu.VMEM((2, aug, tq), F32),
        ],
        compiler_params=_params("arbitrary", "arbitrary", "arbitrary"),
        name="mla_attention",
    )(qn, qr, kn, kr, vt)


def _residual(y, x, gate, g_post):
    return x + gate * _rms(y, g_post)


def _prenorm(x, g_pre, sc, sh):
    return (_rms(x, g_pre) * (1.0 + sc) + sh).astype(BF16)


def _out_proj_kernel(a_ref, w_ref, x_ref, mod_ref, ng_ref, xo_ref, ho_ref, *, sub):
    for r in range(a_ref.shape[0] // sub):
        rs = slice(r * sub, (r + 1) * sub)
        y = _dot(a_ref[rs, :], w_ref[...])
        x1 = _residual(y, x_ref[rs, :], mod_ref[2:3, :], ng_ref[1:2, :])
        xo_ref[rs, :] = x1
        ho_ref[rs, :] = _prenorm(x1, ng_ref[2:3, :], mod_ref[4:5, :], mod_ref[3:4, :])


def _out_proj(a, w, x2, mod, ng, layer, batch):
    t, d = x2.shape
    tm = TM_OUT
    per_seq = (t // batch) // tm
    wb = w.astype(BF16)
    row = lambda width: pl.BlockSpec((tm, width), lambda i: (i, 0))
    return pl.pallas_call(
        functools.partial(_out_proj_kernel, sub=SUB_OUT),
        grid=(t // tm,),
        in_specs=[
            row(a.shape[1]),
            pl.BlockSpec(wb.shape, lambda i: (0, 0)),
            row(d),
            pl.BlockSpec((None, N_MOD, d), lambda i: (layer * batch + i // per_seq, 0, 0)),
            pl.BlockSpec((None, 4, d), lambda i: (layer, 0, 0)),
        ],
        out_specs=[row(d), row(d)],
        out_shape=[jax.ShapeDtypeStruct((t, d), F32), jax.ShapeDtypeStruct((t, d), BF16)],
        compiler_params=_params("arbitrary"),
        name="mixer_out_proj",
    )(a, wb, x2, mod, ng)


def _mlp_kernel(*refs, has_next, sub):
    if has_next:
        h_ref, wu_ref, wd_ref, x_ref, mod_ref, ng_ref, modn_ref, ngn_ref, xo_ref, ho_ref, acc_ref = refs
    else:
        h_ref, wu_ref, wd_ref, x_ref, mod_ref, ng_ref, xo_ref, acc_ref = refs
    k = pl.program_id(1)

    last = pl.num_programs(1) - 1

    def partial_y(rows):
        a = jnp.maximum(_dot(h_ref[rows, :], wu_ref[...]), 0.0)
        return _dot((a * a).astype(BF16), wd_ref[...])

    @pl.when(k == 0)
    def _():
        acc_ref[...] = jnp.zeros(acc_ref.shape, F32)

    @pl.when(k < last)
    def _():
        acc_ref[...] += partial_y(slice(None))

    @pl.when(k == last)
    def _():
        for r in range(h_ref.shape[0] // sub):
            rs = slice(r * sub, (r + 1) * sub)
            y = acc_ref[rs, :] + partial_y(rs)
            x2 = _residual(y, x_ref[rs, :], mod_ref[5:6, :], ng_ref[3:4, :])
            xo_ref[rs, :] = x2
            if has_next:
                ho_ref[rs, :] = _prenorm(x2, ngn_ref[0:1, :], modn_ref[1:2, :], modn_ref[0:1, :])


def _mlp(h, wu, wd, x2, mod, ng, layer, batch, has_next):
    t, d = x2.shape
    d_ff = wu.shape[2]
    tm, tf = TM_MLP, TF_MLP
    per_seq = (t // batch) // tm
    row = lambda: pl.BlockSpec((tm, d), lambda i, k: (i, 0))
    mod_spec = lambda l: pl.BlockSpec((None, N_MOD, d), lambda i, k: (l * batch + i // per_seq, 0, 0))
    ng_spec = lambda l: pl.BlockSpec((None, 4, d), lambda i, k: (l, 0, 0))
    in_specs = [
        row(),
        pl.BlockSpec((None, d, tf), lambda i, k: (layer, 0, k)),
        pl.BlockSpec((None, tf, d), lambda i, k: (layer, k, 0)),
        row(), mod_spec(layer), ng_spec(layer),
    ]
    args = [h, wu, wd, x2, mod, ng]
    out_specs = [row()]
    out_shape = [jax.ShapeDtypeStruct((t, d), F32)]
    if has_next:
        in_specs += [mod_spec(layer + 1), ng_spec(layer + 1)]
        args += [mod, ng]
        out_specs.append(row())
        out_shape.append(jax.ShapeDtypeStruct((t, d), BF16))
    return pl.pallas_call(
        functools.partial(_mlp_kernel, has_next=has_next, sub=SUB_OUT),
        grid=(t // tm, d_ff // tf),
        in_specs=in_specs,
        out_specs=out_specs,
        out_shape=out_shape,
        scratch_shapes=[pltpu.VMEM((tm, d), F32)],
        compiler_params=_params("arbitrary", "arbitrary"),
        name="sq_relu_mlp",
    )(*args)


def _conv_kernel(h_ref, w_ref, cw_ref, g_ref, carry_ref, *, d, per_seq, chunk):
    i = pl.program_id(0)
    h = h_ref[...]
    tm = h.shape[0]
    @pl.when(i % per_seq == 0)
    def _():
        carry_ref[...] = jnp.zeros(carry_ref.shape, F32)

    row = lax.broadcasted_iota(jnp.int32, (tm, chunk), 0)
    for j in range(d // chunk):
        cs = slice(j * chunk, (j + 1) * chunk)
        b_gate = _dot(h, w_ref[:, cs])
        c_gate = _dot(h, w_ref[:, d + j * chunk:d + (j + 1) * chunk])
        u = _dot(h, w_ref[:, 2 * d + j * chunk:2 * d + (j + 1) * chunk])
        z = c_gate * u
        prev = carry_ref[:, cs]
        carry_ref[:, cs] = z[tm - SUBLANES:, :]
        p1 = prev[SUBLANES - 1:SUBLANES, :]
        p2 = prev[SUBLANES - 2:SUBLANES - 1, :]
        z1 = jnp.where(row == 0, p1, pltpu.roll(z, 1, 0))
        z2 = jnp.where(row == 0, p2, jnp.where(row == 1, p1, pltpu.roll(z, 2, 0)))
        conv = cw_ref[0:1, cs] * z2 + cw_ref[1:2, cs] * z1 + cw_ref[2:3, cs] * z
        g_ref[:, cs] = (b_gate * conv).astype(BF16)


def _conv_front(h, w_in, conv_w, batch):
    t, d = h.shape
    tm = TM_CONV
    per_seq = (t // batch) // tm
    wb = w_in.astype(BF16)
    return pl.pallas_call(
        functools.partial(_conv_kernel, d=d, per_seq=per_seq, chunk=TC_CONV),
        grid=(t // tm,),
        in_specs=[
            pl.BlockSpec((tm, d), lambda i: (i, 0)),
            pl.BlockSpec(wb.shape, lambda i: (0, 0)),
            pl.BlockSpec(conv_w.shape, lambda i: (0, 0)),
        ],
        out_specs=pl.BlockSpec((tm, d), lambda i: (i, 0)),
        out_shape=jax.ShapeDtypeStruct((t, d), BF16),
        scratch_shapes=[pltpu.VMEM((SUBLANES, d), F32)],
        compiler_params=_params("arbitrary"),
        name="short_conv_front",
    )(h, wb, conv_w)


def _prenorm_kernel(x_ref, mod_ref, ng_ref, h_ref):
    h_ref[...] = _prenorm(x_ref[...], ng_ref[0:1, :], mod_ref[1:2, :], mod_ref[0:1, :])


def _first_prenorm(x2, mod, ng, layer, batch):
    t, d = x2.shape
    tm = TM_OUT
    per_seq = (t // batch) // tm
    return pl.pallas_call(
        _prenorm_kernel,
        grid=(t // tm,),
        in_specs=[
            pl.BlockSpec((tm, d), lambda i: (i, 0)),
            pl.BlockSpec((None, N_MOD, d), lambda i: (layer * batch + i // per_seq, 0, 0)),
            pl.BlockSpec((None, 4, d), lambda i: (layer, 0, 0)),
        ],
        out_specs=pl.BlockSpec((tm, d), lambda i: (i, 0)),
        out_shape=jax.ShapeDtypeStruct((t, d), BF16),
        compiler_params=_params("arbitrary"),
        name="first_prenorm",
    )(x2, mod, ng)


def kernel(x, c, positions, w_mod, b_mod, norm_g, mla_w_in, mla_g_q, mla_g_kv, mla_w_uq, mla_w_ukv,
           mla_w_o, conv_w_in, conv_w, conv_w_out, mlp_w_up, mlp_w_down):
    batch, seq, d = x.shape
    depth = w_mod.shape[0]
    t = batch * seq
    mod = _modulation(c, w_mod, b_mod).reshape(depth * batch, N_MOD, d)
    rope_dim = mla_w_in.shape[2] - mla_w_uq.shape[1] - mla_w_ukv.shape[1]
    cos, sin = _rope_tables(positions, rope_dim)
    x2 = x.reshape(t, d)
    w_up, w_down = mlp_w_up.astype(BF16), mlp_w_down.astype(BF16)
    h = None
    for i in range(depth):
        j = i // N_MIXERS
        if i % N_MIXERS == 0:
            qn, qr, kn, v, kr = _mla_proj(x2, mod, norm_g, mla_w_in[j], mla_g_q[j], mla_g_kv[j],
                                          mla_w_uq[j], mla_w_ukv[j], cos, sin, i, batch)
            a = _attention(qn, qr, kn, kr, v, batch)
            w_out = mla_w_o[j]
        else:
            if h is None:
                h = _first_prenorm(x2, mod, norm_g, i, batch)
            a = _conv_front(h, conv_w_in[j], conv_w[j], batch)
            w_out = conv_w_out[j]
        x2, h = _out_proj(a, w_out, x2, mod, norm_g, i, batch)
        has_next = i + 1 < depth and (i + 1) % N_MIXERS == 1
        outs = _mlp(h, w_up, w_down, x2, mod, norm_g, i, batch, has_next)
        x2 = outs[0]
        h = outs[1] if has_next else None
    return x2.reshape(batch, seq, d)
```

```python
import functools
import math

import jax
import jax.numpy as jnp
from jax import lax
from jax.experimental import pallas as pl
from jax.experimental.pallas import tpu as pltpu

F32 = jnp.float32
BF16 = jnp.bfloat16

N_MIXERS = 2
N_MOD = 6
NORM_EPS = 1e-6
ROPE_THETA = 10000.0
CONV_WIDTH = 3
LANES = 128
SUBLANES = 8
BF16_ROWS = 16
MASK_VALUE = -1e30

TM_PROJ = 512
TM_OUT = 512
SUB_OUT = 256
TM_MLP = 512
TF_MLP = 1024
TM_CONV = 512
TC_CONV = 512
NSUB_ATTN = 4
TN_MOD = 1024
TM_ROPE = 2048
VMEM_LIMIT = 56 * 1024 * 1024


def _dot(a, b):
    return jnp.dot(a, b, preferred_element_type=F32)


def _dot_nt(a, b):
    return lax.dot_general(a, b, (((1,), (1,)), ((), ())), preferred_element_type=F32)


def _rms(x, g):
    return x * lax.rsqrt(jnp.mean(x * x, axis=-1, keepdims=True) + NORM_EPS) * g


def _params(*sem):
    return pltpu.CompilerParams(dimension_semantics=sem, vmem_limit_bytes=VMEM_LIMIT)


def _side_cast_plan(weights, steps, step_of):
    flats, specs, shapes = [], [], []
    for w in weights:
        flat = w.reshape(-1, w.shape[-1])
        rows = flat.shape[0] // steps
        assert rows * steps == flat.shape[0] and rows % BF16_ROWS == 0
        flats.append(flat)
        specs.append(pl.BlockSpec((rows, flat.shape[1]), lambda *g: (step_of(*g), 0)))
        shapes.append(jax.ShapeDtypeStruct(flat.shape, BF16))
    return flats, specs, shapes


def _side_cast(src_refs, dst_refs):
    for src, dst in zip(src_refs, dst_refs):
        dst[...] = src[...].astype(BF16)


def _mod_kernel(c_ref, w_ref, b_ref, o_ref):
    c = c_ref[...]
    cond = c / (1.0 + jnp.exp(-c))
    o_ref[...] = _dot(cond.astype(BF16), w_ref[...].astype(BF16)) + b_ref[...]


def _modulation(c, w_mod, b_mod):
    depth, d, n = w_mod.shape
    b = c.shape[0]
    return pl.pallas_call(
        _mod_kernel,
        grid=(depth, n // TN_MOD),
        in_specs=[
            pl.BlockSpec((b, d), lambda l, j: (0, 0)),
            pl.BlockSpec((None, d, TN_MOD), lambda l, j: (l, 0, j)),
            pl.BlockSpec((None, 1, TN_MOD), lambda l, j: (l, 0, j)),
        ],
        out_specs=pl.BlockSpec((None, b, TN_MOD), lambda l, j: (l, 0, j)),
        out_shape=jax.ShapeDtypeStruct((depth, b, n), F32),
        compiler_params=_params("arbitrary", "arbitrary"),
        name="adaln_mod",
    )(c, w_mod, b_mod.reshape(depth, 1, n))


def _rope_kernel(pos_ref, invf_ref, cos_ref, sin_ref):
    ang = pos_ref[...].astype(F32) * invf_ref[...]
    cos_ref[...] = jnp.cos(ang)
    sin_ref[...] = jnp.sin(ang)


def _rope_tables(positions, rope_dim):
    t = positions.size
    half = rope_dim // 2
    inv_freq = ROPE_THETA ** (-jnp.arange(0, rope_dim, 2, dtype=F32) / rope_dim)
    inv_freq = jnp.tile(inv_freq, LANES // half).reshape(1, LANES)
    out = jax.ShapeDtypeStruct((t, LANES), F32)
    return pl.pallas_call(
        _rope_kernel,
        grid=(t // TM_ROPE,),
        in_specs=[
            pl.BlockSpec((TM_ROPE, 1), lambda i: (i, 0)),
            pl.BlockSpec((1, LANES), lambda i: (0, 0)),
        ],
        out_specs=[pl.BlockSpec((TM_ROPE, LANES), lambda i: (i, 0))] * 2,
        out_shape=[out, out],
        compiler_params=_params("arbitrary"),
        name="rope_tables",
    )(positions.reshape(t, 1), inv_freq)


def _mla_proj_kernel(*refs, n_cast, q_rank, kv_rank, nope_w, rope_w, q_scale, chunk):
    (x_ref, mod_ref, ng_ref, win_ref, gq_ref, gkv_ref, wq_ref, wkn_ref, wvt_ref, cos_ref, sin_ref) = refs[:11]
    cast_src = refs[11:11 + n_cast]
    qn_ref, qr_ref, kn_ref, vt_ref, kr_ref = refs[11 + n_cast:16 + n_cast]
    _side_cast(cast_src, refs[16 + n_cast:])
    sh, sc = mod_ref[0:1, :], mod_ref[1:2, :]
    h = (_rms(x_ref[...], ng_ref[0:1, :]) * (1.0 + sc) + sh).astype(BF16)
    lat = _dot(h, win_ref[...])
    cq = _rms(lat[:, :q_rank], gq_ref[...]).astype(BF16)
    ckv = _rms(lat[:, q_rank:q_rank + kv_rank], gkv_ref[...]).astype(BF16)
    cos, sin = cos_ref[...], sin_ref[...]
    o = q_rank + kv_rank
    krr = lat[:, o:o + LANES] * cos + lat[:, o + LANES:o + 2 * LANES] * sin
    lane = lax.broadcasted_iota(jnp.int32, krr.shape, 1)
    zero = jnp.zeros_like(krr)
    kr_ref[:, :LANES] = jnp.where(lane < LANES // 2, krr, zero).astype(BF16)
    kr_ref[:, LANES:] = jnp.where(lane >= LANES // 2, krr, zero).astype(BF16)
    for j in range(nope_w // chunk):
        cs = slice(j * chunk, (j + 1) * chunk)
        qn_ref[:, cs] = (_dot(cq, wq_ref[:, cs]) * q_scale).astype(BF16)
    for j in range(rope_w // LANES):
        a = _dot(cq, wq_ref[:, nope_w + j * LANES:nope_w + (j + 1) * LANES])
        r = _dot(cq, wq_ref[:, nope_w + rope_w + j * LANES:nope_w + rope_w + (j + 1) * LANES])
        qr_ref[:, j * LANES:(j + 1) * LANES] = ((a * cos + r * sin) * q_scale).astype(BF16)
    for j in range(nope_w // chunk):
        cs = slice(j * chunk, (j + 1) * chunk)
        kn_ref[:, cs] = _dot(ckv, wkn_ref[:, cs]).astype(BF16)
        vt_ref[cs, :] = _dot_nt(wvt_ref[cs, :], ckv).astype(BF16)


def _rot_half_weight(w):
    half = w.shape[-1] // 2
    return jnp.concatenate([-w[..., half:], w[..., :half]], axis=-1)


def _mla_proj(x2, mod, ng, w_in, g_q, g_kv, w_uq, w_ukv, cos, sin, layer, batch, cast_weights):
    t, d = x2.shape
    q_rank, heads, qk_dim = w_uq.shape
    kv_rank = w_ukv.shape[0]
    rope = w_in.shape[1] - q_rank - kv_rank
    nope = qk_dim - rope
    vdim = w_ukv.shape[2] - nope
    assert nope == LANES and vdim == LANES and 2 * rope == LANES
    nope_w, rope_w = heads * nope, heads * rope
    w_kr = w_in[:, q_rank + kv_rank:]
    w_kr_r = _rot_half_weight(w_kr)
    win = jnp.concatenate([w_in[:, :q_rank + kv_rank], w_kr, w_kr, w_kr_r, w_kr_r], axis=1).astype(BF16)
    wq_r = w_uq[:, :, nope:]
    wq = jnp.concatenate([w_uq[:, :, :nope].reshape(q_rank, nope_w),
                          wq_r.reshape(q_rank, rope_w),
                          _rot_half_weight(wq_r).reshape(q_rank, rope_w)], axis=1).astype(BF16)
    wkn = w_ukv[:, :, :nope].reshape(kv_rank, nope_w).astype(BF16)
    wvt = w_ukv[:, :, nope:].reshape(kv_rank, nope_w).T.astype(BF16)
    tm = TM_PROJ
    per_seq = (t // batch) // tm
    q_scale = float(qk_dim) ** -0.5 * math.log2(math.e)
    full = lambda a: pl.BlockSpec(a.shape, lambda i: (0,) * a.ndim)
    row = lambda w: pl.BlockSpec((tm, w), lambda i: (i, 0))
    flats, cast_specs, cast_shapes = _side_cast_plan(cast_weights, t // tm, lambda i: i)
    kern = functools.partial(_mla_proj_kernel, n_cast=len(flats), q_rank=q_rank, kv_rank=kv_rank,
                             nope_w=nope_w, rope_w=rope_w, q_scale=q_scale, chunk=512)
    g_q2, g_kv2 = g_q.reshape(1, -1), g_kv.reshape(1, -1)
    outs = pl.pallas_call(
        kern,
        grid=(t // tm,),
        in_specs=[
            row(d),
            pl.BlockSpec((None, N_MOD, d), lambda i: (layer * batch + i // per_seq, 0, 0)),
            pl.BlockSpec((None, 4, d), lambda i: (layer, 0, 0)),
            full(win), full(g_q2), full(g_kv2), full(wq), full(wkn), full(wvt),
            row(LANES), row(LANES),
        ] + cast_specs,
        out_specs=[row(nope_w), row(rope_w), row(nope_w),
                   pl.BlockSpec((None, nope_w, tm), lambda i: (i, 0, 0)), row(2 * LANES)] + cast_specs,
        out_shape=[jax.ShapeDtypeStruct((t, nope_w), BF16), jax.ShapeDtypeStruct((t, rope_w), BF16),
                   jax.ShapeDtypeStruct((t, nope_w), BF16), jax.ShapeDtypeStruct((t // tm, nope_w, tm), BF16),
                   jax.ShapeDtypeStruct((t, 2 * LANES), BF16)] + cast_shapes,
        compiler_params=_params("arbitrary"),
        name="mla_proj",
    )(x2, mod, ng, win, g_q2, g_kv2, wq, wkn, wvt, cos, sin, *flats)
    return outs[:5], [o.reshape(w.shape) for o, w in zip(outs[5:], cast_weights)]


def _attn_kernel(*refs, n_cast, tk, nsub):
    qn_ref, qr_ref, kn_ref, kr_ref, vt_ref = refs[:5]
    o_ref = refs[5 + n_cast]
    kcat_ref, vaug_ref, m_ref, acc_ref = refs[6 + 2 * n_cast:]
    _side_cast(refs[5:5 + n_cast], refs[6 + n_cast:6 + 2 * n_cast])
    qi = pl.program_id(2)
    aug = vaug_ref.shape[2]

    @pl.when(qi == 0)
    def _():
        pad = (lax.broadcasted_iota(jnp.int32, (aug - LANES, tk), 0) == 0).astype(BF16)
        for e in range(2):
            kcat_ref[e, :, :LANES] = kn_ref[:, e * LANES:(e + 1) * LANES]
            kcat_ref[e, :, LANES:] = kr_ref[:, e * LANES:(e + 1) * LANES]
            for t in range(vt_ref.shape[0]):
                vaug_ref[e, t, :LANES, :] = vt_ref[t, e * LANES:(e + 1) * LANES, :]
                vaug_ref[e, t, LANES:, :] = pad

    qr = qr_ref[...]
    qs = [jnp.concatenate([qn_ref[:, e * LANES:(e + 1) * LANES], qr], axis=1) for e in range(2)]
    m_ref[...] = jnp.full(m_ref.shape, MASK_VALUE, F32)
    acc_ref[...] = jnp.zeros(acc_ref.shape, F32)

    def step(j, modes):
        off = pl.multiple_of(j * tk, tk)
        chains = [(e, u) for e in range(2) for u in range(nsub) if modes[u] is not None]
        scores = [_dot_nt(kcat_ref[e, pl.ds(off, tk), :], qs[e][u * tk:(u + 1) * tk, :]) for e, u in chains]
        for (e, u), s in zip(chains, scores):
            cs = slice(u * tk, (u + 1) * tk)
            if modes[u] == "diag":
                key = lax.broadcasted_iota(jnp.int32, s.shape, 0)
                qry = lax.broadcasted_iota(jnp.int32, s.shape, 1)
                s = jnp.where(key <= qry, s, MASK_VALUE)
            m_prev = m_ref[e, :, cs]
            m_new = jnp.maximum(m_prev, jnp.max(s, axis=0, keepdims=True))
            alpha = jnp.exp2(m_prev - m_new)
            p = jnp.exp2(s - m_new).astype(BF16)
            acc_ref[e, :, cs] = acc_ref[e, :, cs] * alpha + _dot(vaug_ref[e, j], p)
            m_ref[e, :, cs] = m_new

    def body(j, carry):
        step(j, ["full"] * nsub)
        return carry

    lax.fori_loop(0, qi * nsub, body, 0)
    for d in range(nsub):
        step(qi * nsub + d, [None if u < d else ("diag" if u == d else "full") for u in range(nsub)])
    for e in range(2):
        o_t = acc_ref[e, :LANES, :] / acc_ref[e, LANES:LANES + 1, :]
        o_ref[:, e * LANES:(e + 1) * LANES] = o_t.T.astype(BF16)


def _attention(qn, qr, kn, kr, vt, batch, cast_weights):
    t, hw = qn.shape
    s = t // batch
    tk = vt.shape[2]
    nsub = NSUB_ATTN
    tq = nsub * tk
    nq, nk = s // tq, s // tk
    pair = 2 * LANES
    npair = hw // pair
    aug = LANES + BF16_ROWS
    flats, cast_specs, cast_shapes = _side_cast_plan(
        cast_weights, batch * npair * nq, lambda b, p, i: (b * npair + p) * nq + i)
    outs = pl.pallas_call(
        functools.partial(_attn_kernel, n_cast=len(flats), tk=tk, nsub=nsub),
        grid=(batch, npair, nq),
        in_specs=[
            pl.BlockSpec((tq, pair), lambda b, p, i: (b * nq + i, p)),
            pl.BlockSpec((tq, LANES), lambda b, p, i: (b * nq + i, p)),
            pl.BlockSpec((s, pair), lambda b, p, i: (b, p)),
            pl.BlockSpec((s, pair), lambda b, p, i: (b, 0)),
            pl.BlockSpec((nk, pair, tk), lambda b, p, i: (b, p, 0)),
        ] + cast_specs,
        out_specs=[pl.BlockSpec((tq, pair), lambda b, p, i: (b * nq + i, p))] + cast_specs,
        out_shape=[jax.ShapeDtypeStruct((t, hw), BF16)] + cast_shapes,
        scratch_shapes=[
            pltpu.VMEM((2, s, pair), BF16),
            pltpu.VMEM((2, nk, aug, tk), BF16),
            pltpu.VMEM((2, 1, tq), F32),
            pltpu.VMEM((2, aug, tq), F32),
        ],
        compiler_params=_params("arbitrary", "arbitrary", "arbitrary"),
        name="mla_attention",
    )(qn, qr, kn, kr, vt, *flats)
    return outs[0], [o.reshape(w.shape) for o, w in zip(outs[1:], cast_weights)]


def _residual(y, x, gate, g_post):
    return x + gate * _rms(y, g_post)


def _prenorm(x, g_pre, sc, sh):
    return (_rms(x, g_pre) * (1.0 + sc) + sh).astype(BF16)


def _out_proj_kernel(a_ref, w_ref, x_ref, mod_ref, ng_ref, xo_ref, ho_ref, *, sub):
    for r in range(a_ref.shape[0] // sub):
        rs = slice(r * sub, (r + 1) * sub)
        y = _dot(a_ref[rs, :], w_ref[...])
        x1 = _residual(y, x_ref[rs, :], mod_ref[2:3, :], ng_ref[1:2, :])
        xo_ref[rs, :] = x1
        ho_ref[rs, :] = _prenorm(x1, ng_ref[2:3, :], mod_ref[4:5, :], mod_ref[3:4, :])


def _out_proj(a, wb, x2, mod, ng, layer, batch):
    t, d = x2.shape
    tm = TM_OUT
    per_seq = (t // batch) // tm
    row = lambda width: pl.BlockSpec((tm, width), lambda i: (i, 0))
    return pl.pallas_call(
        functools.partial(_out_proj_kernel, sub=SUB_OUT),
        grid=(t // tm,),
        in_specs=[
            row(a.shape[1]),
            pl.BlockSpec(wb.shape, lambda i: (0, 0)),
            row(d),
            pl.BlockSpec((None, N_MOD, d), lambda i: (layer * batch + i // per_seq, 0, 0)),
            pl.BlockSpec((None, 4, d), lambda i: (layer, 0, 0)),
        ],
        out_specs=[row(d), row(d)],
        out_shape=[jax.ShapeDtypeStruct((t, d), F32), jax.ShapeDtypeStruct((t, d), BF16)],
        compiler_params=_params("arbitrary"),
        name="mixer_out_proj",
    )(a, wb, x2, mod, ng)


def _mlp_kernel(*refs, has_next, sub, n_chunks):
    if has_next:
        h_ref, wu_ref, wd_ref, x_ref, mod_ref, ng_ref, modn_ref, ngn_ref, xo_ref, ho_ref, acc_ref = refs
    else:
        h_ref, wu_ref, wd_ref, x_ref, mod_ref, ng_ref, xo_ref, acc_ref = refs
    k = pl.program_id(1)

    last = n_chunks - 1

    def partial_y(rows):
        a = jnp.maximum(_dot(h_ref[rows, :], wu_ref[...]), 0.0)
        return _dot((a * a).astype(BF16), wd_ref[...])

    if n_chunks > 1:
        @pl.when(k == 0)
        def _():
            acc_ref[...] = partial_y(slice(None))

    if n_chunks > 2:
        @pl.when(jnp.logical_and(k > 0, k < last))
        def _():
            acc_ref[...] += partial_y(slice(None))

    @pl.when(k == last)
    def _():
        for r in range(h_ref.shape[0] // sub):
            rs = slice(r * sub, (r + 1) * sub)
            y = partial_y(rs)
            if n_chunks > 1:
                y = acc_ref[rs, :] + y
            x2 = _residual(y, x_ref[rs, :], mod_ref[5:6, :], ng_ref[3:4, :])
            xo_ref[rs, :] = x2
            if has_next:
                ho_ref[rs, :] = _prenorm(x2, ngn_ref[0:1, :], modn_ref[1:2, :], modn_ref[0:1, :])


def _mlp(h, wu, wd, x2, mod, ng, layer, batch, has_next):
    t, d = x2.shape
    d_ff = wu.shape[2]
    tm, tf = TM_MLP, TF_MLP
    per_seq = (t // batch) // tm
    row = lambda: pl.BlockSpec((tm, d), lambda i, k: (i, 0))
    mod_spec = lambda l: pl.BlockSpec((None, N_MOD, d), lambda i, k: (l * batch + i // per_seq, 0, 0))
    ng_spec = lambda l: pl.BlockSpec((None, 4, d), lambda i, k: (l, 0, 0))
    in_specs = [
        row(),
        pl.BlockSpec((None, d, tf), lambda i, k: (layer, 0, k)),
        pl.BlockSpec((None, tf, d), lambda i, k: (layer, k, 0)),
        row(), mod_spec(layer), ng_spec(layer),
    ]
    args = [h, wu, wd, x2, mod, ng]
    out_specs = [row()]
    out_shape = [jax.ShapeDtypeStruct((t, d), F32)]
    if has_next:
        in_specs += [mod_spec(layer + 1), ng_spec(layer + 1)]
        args += [mod, ng]
        out_specs.append(row())
        out_shape.append(jax.ShapeDtypeStruct((t, d), BF16))
    return pl.pallas_call(
        functools.partial(_mlp_kernel, has_next=has_next, sub=SUB_OUT, n_chunks=d_ff // tf),
        grid=(t // tm, d_ff // tf),
        in_specs=in_specs,
        out_specs=out_specs,
        out_shape=out_shape,
        scratch_shapes=[pltpu.VMEM((tm, d), F32)],
        compiler_params=_params("arbitrary", "arbitrary"),
        name="sq_relu_mlp",
    )(*args)


def _conv_kernel(h_ref, w_ref, cw_ref, g_ref, carry_ref, *, d, per_seq, chunk):
    i = pl.program_id(0)
    h = h_ref[...]
    tm = h.shape[0]
    @pl.when(i % per_seq == 0)
    def _():
        carry_ref[...] = jnp.zeros(carry_ref.shape, F32)

    row = lax.broadcasted_iota(jnp.int32, (tm, chunk), 0)
    for j in range(d // chunk):
        cs = slice(j * chunk, (j + 1) * chunk)
        b_gate = _dot(h, w_ref[:, cs])
        c_gate = _dot(h, w_ref[:, d + j * chunk:d + (j + 1) * chunk])
        u = _dot(h, w_ref[:, 2 * d + j * chunk:2 * d + (j + 1) * chunk])
        z = c_gate * u
        prev = carry_ref[:, cs]
        carry_ref[:, cs] = z[tm - SUBLANES:, :]
        p1 = prev[SUBLANES - 1:SUBLANES, :]
        p2 = prev[SUBLANES - 2:SUBLANES - 1, :]
        z1 = jnp.where(row == 0, p1, pltpu.roll(z, 1, 0))
        z2 = jnp.where(row == 0, p2, jnp.where(row == 1, p1, pltpu.roll(z, 2, 0)))
        conv = cw_ref[0:1, cs] * z2 + cw_ref[1:2, cs] * z1 + cw_ref[2:3, cs] * z
        g_ref[:, cs] = (b_gate * conv).astype(BF16)


def _conv_front(h, wb, conv_w, batch):
    t, d = h.shape
    tm = TM_CONV
    per_seq = (t // batch) // tm
    return pl.pallas_call(
        functools.partial(_conv_kernel, d=d, per_seq=per_seq, chunk=TC_CONV),
        grid=(t // tm,),
        in_specs=[
            pl.BlockSpec((tm, d), lambda i: (i, 0)),
            pl.BlockSpec(wb.shape, lambda i: (0, 0)),
            pl.BlockSpec(conv_w.shape, lambda i: (0, 0)),
        ],
        out_specs=pl.BlockSpec((tm, d), lambda i: (i, 0)),
        out_shape=jax.ShapeDtypeStruct((t, d), BF16),
        scratch_shapes=[pltpu.VMEM((SUBLANES, d), F32)],
        compiler_params=_params("arbitrary"),
        name="short_conv_front",
    )(h, wb, conv_w)


def _prenorm_kernel(x_ref, mod_ref, ng_ref, h_ref):
    h_ref[...] = _prenorm(x_ref[...], ng_ref[0:1, :], mod_ref[1:2, :], mod_ref[0:1, :])


def _first_prenorm(x2, mod, ng, layer, batch):
    t, d = x2.shape
    tm = TM_OUT
    per_seq = (t // batch) // tm
    return pl.pallas_call(
        _prenorm_kernel,
        grid=(t // tm,),
        in_specs=[
            pl.BlockSpec((tm, d), lambda i: (i, 0)),
            pl.BlockSpec((None, N_MOD, d), lambda i: (layer * batch + i // per_seq, 0, 0)),
            pl.BlockSpec((None, 4, d), lambda i: (layer, 0, 0)),
        ],
        out_specs=pl.BlockSpec((tm, d), lambda i: (i, 0)),
        out_shape=jax.ShapeDtypeStruct((t, d), BF16),
        compiler_params=_params("arbitrary"),
        name="first_prenorm",
    )(x2, mod, ng)


def kernel(x, c, positions, w_mod, b_mod, norm_g, mla_w_in, mla_g_q, mla_g_kv, mla_w_uq, mla_w_ukv,
           mla_w_o, conv_w_in, conv_w, conv_w_out, mlp_w_up, mlp_w_down):
    batch, seq, d = x.shape
    depth = w_mod.shape[0]
    t = batch * seq
    mod = _modulation(c, w_mod, b_mod).reshape(depth * batch, N_MOD, d)
    rope_dim = mla_w_in.shape[2] - mla_w_uq.shape[1] - mla_w_ukv.shape[1]
    cos, sin = _rope_tables(positions, rope_dim)
    x2 = x.reshape(t, d)
    proj_casts = [w for w in (mla_w_o, conv_w_in, conv_w_out) if w.shape[0]]
    h = None
    for i in range(depth):
        j = i // N_MIXERS
        if i % N_MIXERS == 0:
            (qn, qr, kn, vt, kr), cast = _mla_proj(x2, mod, norm_g, mla_w_in[j], mla_g_q[j], mla_g_kv[j],
                                                   mla_w_uq[j], mla_w_ukv[j], cos, sin, i, batch,
                                                   proj_casts if i == 0 else [])
            if i == 0:
                w_o, w_cin, w_cout = (cast + [None, None])[:3]
            a, cast = _attention(qn, qr, kn, kr, vt, batch, [mlp_w_up, mlp_w_down] if i == 0 else [])
            if i == 0:
                w_up, w_down = cast
            w_out = w_o[j]
        else:
            if h is None:
                h = _first_prenorm(x2, mod, norm_g, i, batch)
            a = _conv_front(h, w_cin[j], conv_w[j], batch)
            w_out = w_cout[j]
        x2, h = _out_proj(a, w_out, x2, mod, norm_g, i, batch)
        has_next = i + 1 < depth and (i + 1) % N_MIXERS == 1
        outs = _mlp(h, w_up, w_down, x2, mod, norm_g, i, batch, has_next)
        x2 = outs[0]
        h = outs[1] if has_next else None
    return x2.reshape(batch, seq, d)
```
